```python
import math
import jax, jax.numpy as jnp
from jax import lax
import numpy as np

D_MODEL = 1024
BATCH = 8
SEQ = 4096
DEPTH = 2

GRID_W = 64
N_GROUPS = 4
MIX_WIDTH = D_MODEL
GROUP_W = MIX_WIDTH // N_GROUPS
CONV_K = 31
HY_ORDER = 2
HY_SHORT_K = 3
HY_BANDS = 16
HY_EMB_DIM = 1 + 2 * HY_BANDS
HY_FILTER_W = 64
HY_FAST_DECAY_PCT = 0.3
HY_SLOW_DECAY_PCT = 1.5
HY_DECAY_TARGET = 1e-2
HY_MOD_SHIFT = 0.05
S5_CH = 16
S5_GROUPS = GROUP_W // S5_CH
S5_STATE = 64
S5_DT_MIN = 1e-3
S5_DT_MAX = 1e-1
HEAD_DIM = 64
N_Q_HEADS = GROUP_W // HEAD_DIM
N_KV_HEADS = 2
Q_PER_KV = N_Q_HEADS // N_KV_HEADS
ROPE_AXIS_DIM = HEAD_DIM // 2
ROPE_THETA = 10000.0
Q_BLOCK = 128
N_EXPERTS = 16
EC_CAPACITY_FACTOR = 2
EXPERT_FF = D_MODEL
DN_ALPHA = (2 * DEPTH) ** 0.25
DN_BETA = (8 * DEPTH) ** -0.25
LN_EPS = 1e-5
RMS_EPS = 1e-6
CONV_IN = 2 * GROUP_W
HY_IN = (HY_ORDER + 1) * GROUP_W
S5_IN = GROUP_W
Q_IN = N_Q_HEADS * HEAD_DIM
KV_IN = N_KV_HEADS * HEAD_DIM
IN_WIDTH = CONV_IN + HY_IN + S5_IN + Q_IN + 2 * KV_IN
IN_SPLITS = (CONV_IN, CONV_IN + HY_IN, CONV_IN + HY_IN + S5_IN, CONV_IN + HY_IN + S5_IN + Q_IN, CONV_IN + HY_IN + S5_IN + Q_IN + KV_IN)

kernel_name = "hybrid_parallel_group_encoder"

F32 = jnp.float32


def _layernorm(x, g, b):
    xf = x.astype(F32)
    mu = jnp.mean(xf, axis=-1, keepdims=True)
    var = jnp.mean(jnp.square(xf - mu), axis=-1, keepdims=True)
    return ((xf - mu) * lax.rsqrt(var + LN_EPS) * g.astype(F32) + b.astype(F32)).astype(x.dtype)


def _rms(xf, g):
    return xf * lax.rsqrt(jnp.mean(jnp.square(xf), axis=-1, keepdims=True) + RMS_EPS) * g.astype(F32)


def _depthwise_conv(x, w, b):
    c = x.shape[-1]
    y = lax.conv_general_dilated(x, w[:, None, :].astype(x.dtype), window_strides=(1,), padding="SAME",
                                 dimension_numbers=("NWC", "WIO", "NWC"), feature_group_count=c)
    return y + b.astype(x.dtype)


def _conformer_conv(u, dw_w, dw_b, ln_g, ln_b):
    a, gate = jnp.split(u, 2, axis=-1)
    h = a * jax.nn.sigmoid(gate)
    h = _depthwise_conv(h, dw_w, dw_b)
    h = _layernorm(h, ln_g, ln_b)
    return jax.nn.silu(h)


def _hyena_pos_features(L):
    t = jnp.linspace(0.0, 1.0, L, dtype=F32)[:, None]
    w = (2.0 * math.pi / L) * jnp.arange(L, dtype=F32)
    f = jnp.linspace(1e-4, HY_BANDS - 1, HY_BANDS, dtype=F32)
    ang = w[:, None] * f[None, :]
    z = jnp.concatenate([t, jnp.cos(ang), -jnp.sin(ang)], axis=-1)
    max_decay = math.log(HY_DECAY_TARGET) / HY_FAST_DECAY_PCT
    min_decay = math.log(HY_DECAY_TARGET) / HY_SLOW_DECAY_PCT
    deltas = jnp.linspace(min_decay, max_decay, GROUP_W, dtype=F32)
    window = jnp.exp(-t * jnp.abs(deltas)[None, :]) + HY_MOD_SHIFT
    return z, window


def _hyena_filter_spectra(z, window, w1, b1, freq, w2, b2, w3):
    L = z.shape[0]
    fr = freq.astype(F32)
    hid = jnp.sin(fr * (z @ w1.astype(F32) + b1.astype(F32)))
    hid = jnp.sin(fr * (hid @ w2.astype(F32) + b2.astype(F32)))
    h = (hid @ w3.astype(F32)).reshape(L, HY_ORDER, 2, GROUP_W) * window[:, None, None, :]
    fwd, bwd = h[:, :, 0], h[:, :, 1]
    k2 = jnp.concatenate([fwd, jnp.zeros((1, HY_ORDER, GROUP_W), F32), bwd[:0:-1]], axis=0)
    return jnp.fft.rfft(k2, axis=0)


def _hyena(u, short_w, short_b, spectra, bias):
    L = u.shape[1]
    uc = _depthwise_conv(u, short_w, short_b)
    v, x1, x2 = jnp.split(uc, HY_ORDER + 1, axis=-1)
    z = v.astype(F32)
    for o, gate in enumerate((x1, x2)):
        zf = jnp.fft.rfft(z, n=2 * L, axis=1)
        conv = jnp.fft.irfft(zf * spectra[None, :, o, :], n=2 * L, axis=1)[:, :L]
        z = gate.astype(F32) * (conv + z * bias[o].astype(F32))
    return z.astype(u.dtype)


def _complex_affine_combine(e1, e2):
    a1r, a1i, b1r, b1i = e1
    a2r, a2i, b2r, b2i = e2
    return (a2r * a1r - a2i * a1i, a2r * a1i + a2i * a1r,
            a2r * b1r - a2i * b1i + b2r, a2r * b1i + a2i * b1r + b2i)


def _s5_direction(uf, lam_re, lam_im, log_step, b_re, b_im, c_re, c_im, reverse):
    lam_re = jnp.minimum(lam_re.astype(F32), -1e-4)
    lam_im = lam_im.astype(F32)
    b_re, b_im, c_re, c_im = (a.astype(F32) for a in (b_re, b_im, c_re, c_im))
    step = jnp.exp(log_step.astype(F32))[:, None]
    mag = jnp.exp(lam_re * step)
    a_re = mag * jnp.cos(lam_im * step)
    a_im = mag * jnp.sin(lam_im * step)
    den = lam_re * lam_re + lam_im * lam_im
    n_re = a_re - 1.0
    z_re = (n_re * lam_re + a_im * lam_im) / den
    z_im = (a_im * lam_re - n_re * lam_im) / den
    bb_re = z_re[..., None] * b_re - z_im[..., None] * b_im
    bb_im = z_re[..., None] * b_im + z_im[..., None] * b_re
    bu_re = jnp.einsum('blgh,gph->blgp', uf, bb_re)
    bu_im = jnp.einsum('blgh,gph->blgp', uf, bb_im)
    shape = bu_re.shape
    elems = (jnp.broadcast_to(a_re, shape), jnp.broadcast_to(a_im, shape), bu_re, bu_im)
    _, _, s_re, s_im = lax.associative_scan(_complex_affine_combine, elems, reverse=reverse, axis=1)
    return jnp.einsum('blgp,ghp->blgh', s_re, c_re) - jnp.einsum('blgp,ghp->blgh', s_im, c_im)


def _s5(u, lam_re, lam_im, log_step, b_re, b_im, c_re, c_im, d, w_glu, b_glu):
    bsz, L, _ = u.shape
    uf = u.astype(F32).reshape(bsz, L, S5_GROUPS, S5_CH)
    y = d.astype(F32).reshape(S5_GROUPS, S5_CH) * uf
    for direction in range(2):
        y = y + _s5_direction(uf, lam_re[direction], lam_im[direction], log_step[direction],
                              b_re[direction], b_im[direction], c_re[direction], c_im[direction],
                              reverse=(direction == 1))
    g = jax.nn.gelu(y.reshape(bsz, L, GROUP_W)).astype(u.dtype)
    a, gate = jnp.split(g @ w_glu + b_glu, 2, axis=-1)
    return a * jax.nn.sigmoid(gate)


def _axial_rope_tables(L):
    rows = L // GRID_W
    row = jnp.repeat(jnp.arange(rows, dtype=F32), GRID_W)
    col = jnp.tile(jnp.arange(GRID_W, dtype=F32), rows)
    inv = ROPE_THETA ** (-jnp.arange(0, ROPE_AXIS_DIM, 2, dtype=F32) / ROPE_AXIS_DIM)
    ang_r = (row[:, None] * inv)[:, None, :]
    ang_c = (col[:, None] * inv)[:, None, :]
    return (jnp.cos(ang_r), jnp.sin(ang_r), jnp.cos(ang_c), jnp.sin(ang_c))


def _rotate(x, cos, sin):
    x1, x2 = jnp.split(x, 2, axis=-1)
    return jnp.concatenate([x1 * cos - x2 * sin, x1 * sin + x2 * cos], axis=-1)


def _apply_axial_rope(x, tabs):
    cr, sr, cc, sc = tabs
    return jnp.concatenate([_rotate(x[..., :ROPE_AXIS_DIM], cr, sr),
                            _rotate(x[..., ROPE_AXIS_DIM:], cc, sc)], axis=-1)


def _gqa(uq, uk, uv, q_g, k_g, tabs):
    bsz, L, _ = uq.shape
    q = uq.reshape(bsz, L, N_Q_HEADS, HEAD_DIM).astype(F32)
    k = uk.reshape(bsz, L, N_KV_HEADS, HEAD_DIM).astype(F32)
    v = uv.reshape(bsz, L, N_KV_HEADS, HEAD_DIM)
    q = _apply_axial_rope(_rms(q, q_g), tabs) * (HEAD_DIM ** -0.5)
    k = _apply_axial_rope(_rms(k, k_g), tabs)
    nb = L // Q_BLOCK
    qb = q.reshape(bsz, nb, Q_BLOCK, N_KV_HEADS, Q_PER_KV, HEAD_DIM).transpose(1, 0, 2, 3, 4, 5)

    def attend(q_blk):
        s = jnp.einsum('bqkgd,bskd->bkgqs', q_blk, k)
        p = jax.nn.softmax(s, axis=-1).astype(v.dtype)
        return jnp.einsum('bkgqs,bskd->bqkgd', p, v)

    o = lax.map(attend, qb)
    return o.transpose(1, 0, 2, 3, 4, 5).reshape(bsz, L, Q_IN)


def _expert_choice_moe(h, router_w, router_b, w_gate, w_up, w_down):
    bsz, L, D = h.shape
    cap = EC_CAPACITY_FACTOR * L // N_EXPERTS
    aff = jax.nn.softmax((h @ router_w + router_b).astype(F32), axis=-1)
    gates, idx = lax.top_k(aff.transpose(0, 2, 1), cap)
    xs = jax.vmap(lambda hb, ib: hb[ib])(h, idx)
    hid = jax.nn.silu(jnp.einsum('becd,edf->becf', xs, w_gate)) * jnp.einsum('becd,edf->becf', xs, w_up)
    y = jnp.einsum('becf,efd->becd', hid, w_down) * gates[..., None].astype(h.dtype)
    return jax.vmap(lambda yb, ib: jnp.zeros((L, D), yb.dtype).at[ib.reshape(-1)].add(yb.reshape(-1, D)))(y, idx)


def setup_inputs(seed: int = 0) -> dict:
    key = jax.random.key(seed)
    ks = iter(jax.random.split(key, 48))

    def nrm(shape, scale):
        return scale * jax.random.normal(next(ks), shape, F32)

    def gain(shape):
        return 1.0 + nrm(shape, 0.02)

    x = nrm((BATCH, SEQ, D_MODEL), 1.0)
    ln_in_g = gain((D_MODEL,))
    ln_in_b = nrm((D_MODEL,), 0.02)
    w_in = nrm((DEPTH, D_MODEL, IN_WIDTH), D_MODEL ** -0.5)
    w_in = w_in.at[:, :, IN_WIDTH - KV_IN:].multiply(DN_BETA)
    conv_dw_w = nrm((DEPTH, CONV_K, GROUP_W), CONV_K ** -0.5)
    conv_dw_b = nrm((DEPTH, GROUP_W), 0.02)
    conv_ln_g = gain((DEPTH, GROUP_W))
    conv_ln_b = nrm((DEPTH, GROUP_W), 0.02)
    hy_short_w = nrm((DEPTH, HY_SHORT_K, HY_IN), HY_SHORT_K ** -0.5)
    hy_short_b = nrm((DEPTH, HY_IN), 0.02)
    hy_f_w1 = nrm((DEPTH, HY_EMB_DIM, HY_FILTER_W), HY_EMB_DIM ** -0.5)
    hy_f_b1 = nrm((DEPTH, HY_FILTER_W), 0.1)
    hy_f_freq = gain((DEPTH, HY_FILTER_W))
    hy_f_w2 = nrm((DEPTH, HY_FILTER_W, HY_FILTER_W), HY_FILTER_W ** -0.5)
    hy_f_b2 = nrm((DEPTH, HY_FILTER_W), 0.1)
    hy_f_w3 = nrm((DEPTH, HY_FILTER_W, HY_ORDER * 2 * GROUP_W), 0.1 * HY_FILTER_W ** -0.5)
    hy_bias = nrm((DEPTH, HY_ORDER, GROUP_W), 1.0)
    s5_lam_re = -0.5 + nrm((DEPTH, 2, S5_GROUPS, S5_STATE), 0.01)
    s5_lam_im = math.pi * jnp.arange(S5_STATE, dtype=F32) + nrm((DEPTH, 2, S5_GROUPS, S5_STATE), 0.01)
    s5_log_step = jax.random.uniform(next(ks), (DEPTH, 2, S5_GROUPS), F32, math.log(S5_DT_MIN), math.log(S5_DT_MAX))
    s5_b_re = nrm((DEPTH, 2, S5_GROUPS, S5_STATE, S5_CH), (2 * S5_CH) ** -0.5)
    s5_b_im = nrm((DEPTH, 2, S5_GROUPS, S5_STATE, S5_CH), (2 * S5_CH) ** -0.5)
    s5_c_re = nrm((DEPTH, 2, S5_GROUPS, S5_CH, S5_STATE), 0.5)
    s5_c_im = nrm((DEPTH, 2, S5_GROUPS, S5_CH, S5_STATE), 0.5)
    s5_d = nrm((DEPTH, GROUP_W), 1.0)
    s5_w_glu = nrm((DEPTH, GROUP_W, 2 * GROUP_W), GROUP_W ** -0.5)
    s5_b_glu = nrm((DEPTH, 2 * GROUP_W), 0.02)
    q_norm_g = gain((DEPTH, HEAD_DIM))
    k_norm_g = gain((DEPTH, HEAD_DIM))
    mix_norm_g = gain((DEPTH, MIX_WIDTH))
    w_out = nrm((DEPTH, MIX_WIDTH, D_MODEL), DN_BETA * MIX_WIDTH ** -0.5)
    ln1_g = gain((DEPTH, D_MODEL))
    ln1_b = nrm((DEPTH, D_MODEL), 0.02)
    router_w = nrm((DEPTH, D_MODEL, N_EXPERTS), D_MODEL ** -0.5)
    router_b = nrm((DEPTH, N_EXPERTS), 0.01)
    exp_w_gate = nrm((DEPTH, N_EXPERTS, D_MODEL, EXPERT_FF), D_MODEL ** -0.5)
    exp_w_up = nrm((DEPTH, N_EXPERTS, D_MODEL, EXPERT_FF), D_MODEL ** -0.5)
    exp_w_down = nrm((DEPTH, N_EXPERTS, EXPERT_FF, D_MODEL), DN_BETA * EXPERT_FF ** -0.5)
    ln2_g = gain((DEPTH, D_MODEL))
    ln2_b = nrm((DEPTH, D_MODEL), 0.02)
    return {"x": x, "ln_in_g": ln_in_g, "ln_in_b": ln_in_b, "w_in": w_in,
            "conv_dw_w": conv_dw_w, "conv_dw_b": conv_dw_b, "conv_ln_g": conv_ln_g, "conv_ln_b": conv_ln_b,
            "hy_short_w": hy_short_w, "hy_short_b": hy_short_b, "hy_f_w1": hy_f_w1, "hy_f_b1": hy_f_b1,
            "hy_f_freq": hy_f_freq, "hy_f_w2": hy_f_w2, "hy_f_b2": hy_f_b2, "hy_f_w3": hy_f_w3, "hy_bias": hy_bias,
            "s5_lam_re": s5_lam_re, "s5_lam_im": s5_lam_im, "s5_log_step": s5_log_step,
            "s5_b_re": s5_b_re, "s5_b_im": s5_b_im, "s5_c_re": s5_c_re, "s5_c_im": s5_c_im,
            "s5_d": s5_d, "s5_w_glu": s5_w_glu, "s5_b_glu": s5_b_glu,
            "q_norm_g": q_norm_g, "k_norm_g": k_norm_g, "mix_norm_g": mix_norm_g, "w_out": w_out,
            "ln1_g": ln1_g, "ln1_b": ln1_b, "router_w": router_w, "router_b": router_b,
            "exp_w_gate": exp_w_gate, "exp_w_up": exp_w_up, "exp_w_down": exp_w_down,
            "ln2_g": ln2_g, "ln2_b": ln2_b}


def reference(x, ln_in_g, ln_in_b, w_in, conv_dw_w, conv_dw_b, conv_ln_g, conv_ln_b,
              hy_short_w, hy_short_b, hy_f_w1, hy_f_b1, hy_f_freq, hy_f_w2, hy_f_b2, hy_f_w3, hy_bias,
              s5_lam_re, s5_lam_im, s5_log_step, s5_b_re, s5_b_im, s5_c_re, s5_c_im, s5_d, s5_w_glu, s5_b_glu,
              q_norm_g, k_norm_g, mix_norm_g, w_out, ln1_g, ln1_b, router_w, router_b,
              exp_w_gate, exp_w_up, exp_w_down, ln2_g, ln2_b):
    bsz, L, _ = x.shape
    rope_tabs = _axial_rope_tables(L)
    hy_z, hy_window = _hyena_pos_features(L)
    h = _layernorm(x, ln_in_g, ln_in_b)
    for l in range(DEPTH):
        proj = h @ w_in[l]
        u_conv, u_hy, u_s5, u_q, u_k, u_v = jnp.split(proj, IN_SPLITS, axis=-1)
        out_a = _conformer_conv(u_conv, conv_dw_w[l], conv_dw_b[l], conv_ln_g[l], conv_ln_b[l])
        spectra = _hyena_filter_spectra(hy_z, hy_window, hy_f_w1[l], hy_f_b1[l], hy_f_freq[l],
                                        hy_f_w2[l], hy_f_b2[l], hy_f_w3[l])
        out_b = _hyena(u_hy, hy_short_w[l], hy_short_b[l], spectra, hy_bias[l])
        out_c = _s5(u_s5, s5_lam_re[l], s5_lam_im[l], s5_log_step[l], s5_b_re[l], s5_b_im[l],
                    s5_c_re[l], s5_c_im[l], s5_d[l], s5_w_glu[l], s5_b_glu[l])
        out_d = _gqa(u_q, u_k, u_v, q_norm_g[l], k_norm_g[l], rope_tabs)
        groups = jnp.stack([out_a.astype(F32), out_b.astype(F32), out_c.astype(F32), out_d.astype(F32)], axis=2)
        merged = _rms(groups, mix_norm_g[l].reshape(N_GROUPS, GROUP_W)).reshape(bsz, L, MIX_WIDTH).astype(h.dtype)
        h = _layernorm(DN_ALPHA * h + merged @ w_out[l], ln1_g[l], ln1_b[l])
        moe = _expert_choice_moe(h, router_w[l], router_b[l], exp_w_gate[l], exp_w_up[l], exp_w_down[l])
        h = _layernorm(DN_ALPHA * h + moe, ln2_g[l], ln2_b[l])
    return h
```

```python
import functools
import math

import numpy as np
import jax
import jax.numpy as jnp
from jax import lax
from jax.experimental import pallas as pl
from jax.experimental.pallas import tpu as pltpu

F32 = jnp.float32
BF16 = jnp.bfloat16

GROUP_W = 256
CONV_K = 31
HY_ORDER = 2
HY_BANDS = 16
HY_FAST_DECAY_PCT = 0.3
HY_SLOW_DECAY_PCT = 1.5
HY_DECAY_TARGET = 1e-2
HY_MOD_SHIFT = 0.05
S5_CH = 16
S5_GROUPS = 16
S5_STATE = 64
S5_CHUNK = 16
HEAD_DIM = 64
N_Q_HEADS = 4
N_KV_HEADS = 2
ROPE_AXIS_DIM = 32
ROPE_THETA = 10000.0
GRID_W = 64
N_EXPERTS = 16
EC_CAPACITY_FACTOR = 2
DEPTH = 2
DN_ALPHA = (2 * DEPTH) ** 0.25
LN_EPS = 1e-5
RMS_EPS = 1e-6

VMEM_LIMIT_BYTES = 56 * 1024 * 1024

FFT_A = 64
FFT_B = 128
FFT_SUB = 8


def _cparams(sem):
    return pltpu.CompilerParams(dimension_semantics=sem, vmem_limit_bytes=VMEM_LIMIT_BYTES)


def _ln(x, g, b):
    mu = jnp.mean(x, axis=-1, keepdims=True)
    xc = x - mu
    var = jnp.mean(xc * xc, axis=-1, keepdims=True)
    return xc * lax.rsqrt(var + LN_EPS) * g + b


def _ln_kernel(x_ref, g_ref, b_ref, o_ref):
    o_ref[...] = _ln(x_ref[...], g_ref[...], b_ref[...])


def _layernorm_rows(x, g, b, tm=512):
    n, d = x.shape
    return pl.pallas_call(
        _ln_kernel,
        grid=(n // tm,),
        in_specs=[pl.BlockSpec((tm, d), lambda i: (i, 0)),
                  pl.BlockSpec((1, d), lambda i: (0, 0)),
                  pl.BlockSpec((1, d), lambda i: (0, 0))],
        out_specs=pl.BlockSpec((tm, d), lambda i: (i, 0)),
        out_shape=jax.ShapeDtypeStruct((n, d), F32),
        compiler_params=_cparams(("parallel",)),
    )(x, g.reshape(1, d), b.reshape(1, d))


IN_SEGMENTS = (512, 768, 256, 256, 128, 128)


def _inproj_kernel(h_ref, w_ref, *o_refs):
    hb = h_ref[...].astype(BF16)
    start = 0
    for o_ref, width in zip(o_refs, IN_SEGMENTS):
        o_ref[...] = jnp.dot(hb, w_ref[:, start:start + width], preferred_element_type=F32)
        start += width


def _inproj(h, w_bf16, tm=512):
    n, d = h.shape
    width = w_bf16.shape[1]
    return pl.pallas_call(
        _inproj_kernel,
        grid=(n // tm,),
        in_specs=[pl.BlockSpec((tm, d), lambda i: (i, 0)),
                  pl.BlockSpec((d, width), lambda i: (0, 0))],
        out_specs=[pl.BlockSpec((tm, s), lambda i: (i, 0)) for s in IN_SEGMENTS],
        out_shape=[jax.ShapeDtypeStruct((n, s), F32) for s in IN_SEGMENTS],
        compiler_params=_cparams(("parallel",)),
    )(h, w_bf16)


SUBLANES = 8
CONV_TILE = 64
CONV_PAD = 16


def _conformer_kernel(u_ref, w_ref, b_ref, g_ref, beta_ref, o_ref, pad_ref, *, seq):
    c = GROUP_W
    half = CONV_K // 2
    zeros = jnp.zeros((CONV_PAD, c), F32)
    pad_ref[0:CONV_PAD, :] = zeros
    pad_ref[seq + CONV_PAD:seq + 2 * CONV_PAD, :] = zeros

    def glu_body(t, carry):
        r = pl.multiple_of(t * CONV_TILE, CONV_TILE)
        a = u_ref[0, pl.ds(r, CONV_TILE), 0:c]
        gate = u_ref[0, pl.ds(r, CONV_TILE), c:2 * c]
        pad_ref[pl.ds(r + CONV_PAD, CONV_TILE), :] = a * jax.nn.sigmoid(gate)
        return carry

    lax.fori_loop(0, seq // CONV_TILE, glu_body, 0)

    def conv_body(t, carry):
        r = pl.multiple_of(t * CONV_TILE, CONV_TILE)
        window = pad_ref[pl.ds(r, CONV_TILE + 2 * CONV_PAD), :]
        acc = jnp.zeros((CONV_TILE, c), F32) + b_ref[...]
        for res in range(SUBLANES):
            offsets = [o for o in range(CONV_PAD - half, CONV_PAD - half + CONV_K) if o % SUBLANES == res]
            if not offsets:
                continue
            span = offsets[-1] - res + CONV_TILE
            shifted = window[res:res + span, :]
            for o in offsets:
                k = o - (CONV_PAD - half)
                acc = acc + w_ref[k:k + 1, :] * shifted[o - res:o - res + CONV_TILE, :]
        y = _ln(acc, g_ref[...], beta_ref[...])
        o_ref[0, pl.ds(r, CONV_TILE), :] = y * jax.nn.sigmoid(y)
        return carry

    lax.fori_loop(0, seq // CONV_TILE, conv_body, 0)


def _conformer(u, dw_w, dw_b, ln_g, ln_b):
    bsz, seq, _ = u.shape
    c = GROUP_W
    vec = lambda a: a.reshape(1, c)
    return pl.pallas_call(
        functools.partial(_conformer_kernel, seq=seq),
        grid=(bsz,),
        in_specs=[pl.BlockSpec((1, seq, 2 * c), lambda b: (b, 0, 0)),
                  pl.BlockSpec((CONV_K, c), lambda b: (0, 0)),
                  pl.BlockSpec((1, c), lambda b: (0, 0)),
                  pl.BlockSpec((1, c), lambda b: (0, 0)),
                  pl.BlockSpec((1, c), lambda b: (0, 0))],
        out_specs=pl.BlockSpec((1, seq, c), lambda b: (b, 0, 0)),
        out_shape=jax.ShapeDtypeStruct((bsz, seq, c), F32),
        scratch_shapes=[pltpu.VMEM((seq + 2 * CONV_PAD, c), F32)],
        compiler_params=_cparams(("parallel",)),
    )(u, dw_w, vec(dw_b), vec(ln_g), vec(ln_b))


SHORT_TILE = 64
SHORT_PAD = 8


def _shortconv_kernel(u_ref, w_ref, b_ref, o_ref, pad_ref, *, seq):
    c = u_ref.shape[2]
    zeros = jnp.zeros((SHORT_PAD, c), F32)
    pad_ref[0:SHORT_PAD, :] = zeros
    pad_ref[seq + SHORT_PAD:seq + 2 * SHORT_PAD, :] = zeros

    def copy_body(t, carry):
        r = pl.multiple_of(t * SHORT_TILE, SHORT_TILE)
        pad_ref[pl.ds(r + SHORT_PAD, SHORT_TILE), :] = u_ref[0, pl.ds(r, SHORT_TILE), :]
        return carry

    lax.fori_loop(0, seq // SHORT_TILE, copy_body, 0)

    def conv_body(t, carry):
        r = pl.multiple_of(t * SHORT_TILE, SHORT_TILE)
        window = pad_ref[pl.ds(r, SHORT_TILE + 2 * SHORT_PAD), :]
        acc = b_ref[...] + w_ref[0:1, :] * window[SHORT_PAD - 1:SHORT_PAD - 1 + SHORT_TILE, :]
        acc = acc + w_ref[1:2, :] * window[SHORT_PAD:SHORT_PAD + SHORT_TILE, :]
        acc = acc + w_ref[2:3, :] * window[SHORT_PAD + 1:SHORT_PAD + 1 + SHORT_TILE, :]
        o_ref[0, pl.ds(r, SHORT_TILE), :] = acc
        return carry

    lax.fori_loop(0, seq // SHORT_TILE, conv_body, 0)


def _shortconv(u, w, b):
    bsz, seq, c = u.shape
    cb = 256
    return pl.pallas_call(
        functools.partial(_shortconv_kernel, seq=seq),
        grid=(bsz, c // cb),
        in_specs=[pl.BlockSpec((1, seq, cb), lambda i, j: (i, 0, j)),
                  pl.BlockSpec((3, cb), lambda i, j: (0, j)),
                  pl.BlockSpec((1, cb), lambda i, j: (0, j))],
        out_specs=pl.BlockSpec((1, seq, cb), lambda i, j: (i, 0, j)),
        out_shape=jax.ShapeDtypeStruct((bsz, seq, c), F32),
        scratch_shapes=[pltpu.VMEM((seq + 2 * SHORT_PAD, cb), F32)],
        compiler_params=_cparams(("parallel", "parallel")),
    )(u, w, b.reshape(1, c))


def _fft_constants():
    na, nb, sub = FFT_A, FFT_B, FFT_SUB
    n = na * nb
    half_a = na // 2
    ka = np.arange(na)
    ang = 2.0 * np.pi * np.outer(ka, np.arange(half_a)) / na
    eye = np.eye(sub)
    m1 = np.stack([np.cos(ang), -np.sin(ang)], axis=1)
    m1 = np.einsum("kra,bc->krbac", m1, eye).reshape(na * 2 * sub, half_a * sub)
    m2 = np.stack([np.cos(ang), -np.sin(ang)], axis=1) / n
    m2 = np.einsum("kra,bc->abkrc", m2, eye).reshape(half_a * sub, na * 2 * sub)
    angb = 2.0 * np.pi * np.outer(np.arange(nb), np.arange(nb)) / nb
    fre, fim = np.cos(angb), -np.sin(angb)
    angt = 2.0 * np.pi * np.outer(ka, np.arange(nb)) / n
    tr, ti = np.cos(angt), -np.sin(angt)
    return (jnp.asarray(m1, BF16), jnp.asarray(m2, BF16), jnp.asarray(fre, F32), jnp.asarray(fim, F32),
            jnp.asarray(tr.reshape(na, 1, nb), F32), jnp.asarray(ti.reshape(na, 1, nb), F32))


def _hyena_kernel(z_ref, gate_ref, spec_ref, bias_ref, m1_ref, m2_ref, fre_ref, fim_ref, tr_ref, ti_ref,
                  o_ref, a_ref):
    na, nb, sub = FFT_A, FFT_B, FFT_SUB
    half_a = na // 2

    def stage1(j, carry):
        rows = [z_ref[0, pl.ds(pl.multiple_of(a * nb + j * sub, sub), sub), :] for a in range(half_a)]
        xc = jnp.concatenate(rows, axis=0).astype(BF16)
        ac = jnp.dot(m1_ref[...], xc, preferred_element_type=F32)
        for ka in range(na):
            for ri in range(2):
                r0 = (ka * 2 + ri) * sub
                a_ref[ka, pl.ds(pl.multiple_of(ri * nb + j * sub, sub), sub), :] = ac[r0:r0 + sub, :]
        return carry

    lax.fori_loop(0, nb // sub, stage1, 0)

    def stage2(ka, carry):
        tr = tr_ref[ka]
        ti = ti_ref[ka]
        fre = fre_ref[...]
        fim = fim_ref[...]
        gr = fre * tr - fim * ti
        gi = fre * ti + fim * tr
        g = jnp.concatenate([jnp.concatenate([gr, -gi], axis=1),
                             jnp.concatenate([gi, gr], axis=1)], axis=0)
        x = jnp.dot(g.astype(BF16), a_ref[ka].astype(BF16), preferred_element_type=F32)
        k = spec_ref[ka]
        xr, xi = x[:nb], x[nb:]
        kr, ki = k[:nb], k[nb:]
        y = jnp.concatenate([xr * kr - xi * ki, xr * ki + xi * kr], axis=0)
        a_ref[ka] = jnp.dot(g.T.astype(BF16), y.astype(BF16), preferred_element_type=F32)
        return carry

    lax.fori_loop(0, na, stage2, 0)

    def stage3(j, carry):
        rows = [a_ref[ka, pl.ds(pl.multiple_of(ri * nb + j * sub, sub), sub), :]
                for ka in range(na) for ri in range(2)]
        bc = jnp.concatenate(rows, axis=0).astype(BF16)
        yc = jnp.dot(m2_ref[...], bc, preferred_element_type=F32)
        for a in range(half_a):
            rs = pl.ds(pl.multiple_of(a * nb + j * sub, sub), sub)
            z = z_ref[0, rs, :]
            o_ref[0, rs, :] = gate_ref[0, rs, :] * (yc[a * sub:(a + 1) * sub, :] + z * bias_ref[...])
        return carry

    lax.fori_loop(0, nb // sub, stage3, 0)


def _hyena_order(z, z_col, gate, gate_col, spec, bias, consts):
    bsz, seq, _ = z.shape
    cb = 128
    nblk = GROUP_W // cb
    m1, m2, fre, fim, tr, ti = consts
    const = lambda shape: pl.BlockSpec(shape, lambda j, b: (0,) * len(shape))
    return pl.pallas_call(
        _hyena_kernel,
        grid=(nblk, bsz),
        in_specs=[pl.BlockSpec((1, seq, cb), lambda j, b: (b, 0, z_col + j)),
                  pl.BlockSpec((1, seq, cb), lambda j, b: (b, 0, gate_col + j)),
                  pl.BlockSpec((FFT_A, 2 * FFT_B, cb), lambda j, b: (0, 0, j)),
                  pl.BlockSpec((1, cb), lambda j, b: (0, j)),
                  const(m1.shape), const(m2.shape), const(fre.shape), const(fim.shape),
                  const(tr.shape), const(ti.shape)],
        out_specs=pl.BlockSpec((1, seq, cb), lambda j, b: (b, 0, j)),
        out_shape=jax.ShapeDtypeStruct((bsz, seq, GROUP_W), F32),
        scratch_shapes=[pltpu.VMEM((FFT_A, 2 * FFT_B, cb), F32)],
        compiler_params=_cparams(("parallel", "parallel")),
    )(z, gate, spec, bias.reshape(1, GROUP_W), m1, m2, fre, fim, tr, ti)


def _hyena_filter_spectra(seq, w1, b1, freq, w2, b2, w3):
    t = jnp.linspace(0.0, 1.0, seq, dtype=F32)[:, None]
    w = (2.0 * math.pi / seq) * jnp.arange(seq, dtype=F32)
    f = jnp.linspace(1e-4, HY_BANDS - 1, HY_BANDS, dtype=F32)
    ang = w[:, None] * f[None, :]
    z = jnp.concatenate([t, jnp.cos(ang), -jnp.sin(ang)], axis=-1)
    max_decay = math.log(HY_DECAY_TARGET) / HY_FAST_DECAY_PCT
    min_decay = math.log(HY_DECAY_TARGET) / HY_SLOW_DECAY_PCT
    deltas = jnp.linspace(min_decay, max_decay, GROUP_W, dtype=F32)
    window = jnp.exp(-t * jnp.abs(deltas)[None, :]) + HY_MOD_SHIFT
    hp = lax.Precision.HIGHEST
    hid = jnp.sin(freq * (jnp.dot(z, w1, precision=hp) + b1))
    hid = jnp.sin(freq * (jnp.dot(hid, w2, precision=hp) + b2))
    h = jnp.dot(hid, w3, precision=hp).reshape(seq, HY_ORDER, 2, GROUP_W) * window[:, None, None, :]
    fwd, bwd = h[:, :, 0], h[:, :, 1]
    k2 = jnp.concatenate([fwd, jnp.zeros((1, HY_ORDER, GROUP_W), F32), bwd[:0:-1]], axis=0)
    spec = jnp.fft.rfft(k2, axis=0)
    full = jnp.concatenate([spec, jnp.conj(spec[seq - 1:0:-1])], axis=0)
    full = full.reshape(FFT_B, FFT_A, HY_ORDER, GROUP_W).transpose(2, 1, 0, 3)
    return jnp.concatenate([jnp.real(full), jnp.imag(full)], axis=2).astype(F32)


def _hyena(u_hy, short_w, short_b, spectra, bias, consts):
    uc = _shortconv(u_hy, short_w, short_b)
    z1 = _hyena_order(uc, 0, uc, 2, spectra[0], bias[0], consts)
    return _hyena_order(z1, 0, uc, 4, spectra[1], bias[1], consts)


def _mm_kernel(x_ref, w_ref, o_ref):
    o_ref[...] = jnp.dot(x_ref[...], w_ref[...], preferred_element_type=F32).astype(o_ref.dtype)


def _mm_add_kernel(x_ref, w_ref, r_ref, o_ref):
    o_ref[...] = (r_ref[...] + jnp.dot(x_ref[...], w_ref[...], preferred_element_type=F32)).astype(o_ref.dtype)


def _matmul(x, w, tm, tn, residual=None, out_dtype=F32):
    m, k = x.shape
    n = w.shape[1]
    tm = min(tm, m)
    in_specs = [pl.BlockSpec((tm, k), lambda j, i: (i, 0)),
                pl.BlockSpec((k, tn), lambda j, i: (0, j))]
    args = [x, w]
    kern = _mm_kernel
    if residual is not None:
        in_specs.append(pl.BlockSpec((tm, tn), lambda j, i: (i, j)))
        args.append(residual)
        kern = _mm_add_kernel
    return pl.pallas_call(
        kern,
        grid=(n // tn, m // tm),
        in_specs=in_specs,
        out_specs=pl.BlockSpec((tm, tn), lambda j, i: (i, j)),
        out_shape=jax.ShapeDtypeStruct((m, n), out_dtype),
        compiler_params=_cparams(("parallel", "parallel")),
    )(*args)


def _s5_tables(lam_re, lam_im, log_step, b_re, b_im, c_re, c_im, d):
    t_n, g_n, h_n, p_n = S5_CHUNK, S5_GROUPS, S5_CH, S5_STATE
    m = jnp.arange(t_n + 1, dtype=F32)[:, None, None]
    per_dir = []
    for direction in range(2):
        lre = jnp.minimum(lam_re[direction], -1e-4)
        lim = lam_im[direction]
        step = jnp.exp(log_step[direction])[:, None]
        mag = jnp.exp(lre * step)
        a_re = mag * jnp.cos(lim * step)
        a_im = mag * jnp.sin(lim * step)
        den = lre * lre + lim * lim
        n_re = a_re - 1.0
        z_re = (n_re * lre + a_im * lim) / den
        z_im = (a_im * lre - n_re * lim) / den
        bb_re = z_re[..., None] * b_re[direction] - z_im[..., None] * b_im[direction]
        bb_im = z_re[..., None] * b_im[direction] + z_im[..., None] * b_re[direction]
        pmag = jnp.exp(m * (lre * step)[None])
        pw_re = pmag * jnp.cos(m * (lim * step)[None])
        pw_im = pmag * jnp.sin(m * (lim * step)[None])
        per_dir.append((pw_re, pw_im, bb_re, bb_im, c_re[direction], c_im[direction]))

    hp = lax.Precision.HIGHEST
    eye_g = jnp.eye(g_n, dtype=F32)
    lag_kernels = []
    for pw_re, pw_im, bb_re, bb_im, cr, ci in per_dir:
        ab_re = pw_re[:t_n, :, :, None] * bb_re[None] - pw_im[:t_n, :, :, None] * bb_im[None]
        ab_im = pw_re[:t_n, :, :, None] * bb_im[None] + pw_im[:t_n, :, :, None] * bb_re[None]
        lag_kernels.append(jnp.einsum("gkp,mgph->mgkh", cr, ab_re, precision=hp)
                           - jnp.einsum("gkp,mgph->mgkh", ci, ab_im, precision=hp))
    kf, kb = lag_kernels
    diag_d = d.reshape(g_n, h_n)[:, :, None] * jnp.eye(h_n, dtype=F32)[None]
    k_all = jnp.concatenate([kb[:0:-1], (kf[0] + kb[0] + diag_d)[None], kf[1:]], axis=0)
    tt = jnp.arange(t_n)
    lag_idx = tt[None, :] - tt[:, None] + (t_n - 1)
    kt = k_all[lag_idx]
    kt = kt.transpose(0, 2, 4, 1, 3)
    m_intra = (kt[:, :, :, :, None, :] * eye_g[None, :, None, None, :, None]).reshape(
        t_n * g_n * h_n, t_n * g_n * h_n)

    w_state, w_out, a_chunk = [], [], []
    for direction, (pw_re, pw_im, bb_re, bb_im, cr, ci) in enumerate(per_dir):
        expo = (t_n - 1 - tt) if direction == 0 else tt
        sr = pw_re[expo][:, :, :, None] * bb_re[None] - pw_im[expo][:, :, :, None] * bb_im[None]
        si = pw_re[expo][:, :, :, None] * bb_im[None] + pw_im[expo][:, :, :, None] * bb_re[None]
        st = jnp.stack([sr, si], axis=0).transpose(1, 2, 4, 0, 3)
        w_state.append((st[:, :, :, :, None, :] * eye_g[None, :, None, None, :, None]).reshape(
            t_n * g_n * h_n, 2 * g_n * p_n))
        expo_o = (tt + 1) if direction == 0 else (t_n - tt)
        er = cr[None] * pw_re[expo_o][:, :, None, :] - ci[None] * pw_im[expo_o][:, :, None, :]
        ei = cr[None] * pw_im[expo_o][:, :, None, :] + ci[None] * pw_re[expo_o][:, :, None, :]
        eo = jnp.stack([er, -ei], axis=0).transpose(0, 2, 4, 1, 3)
        w_out.append((eo[:, :, :, :, None, :] * eye_g[None, :, None, None, :, None]).reshape(
            2 * g_n * p_n, t_n * g_n * h_n))
        a_chunk.append(jnp.stack([pw_re[t_n].reshape(1, -1), pw_im[t_n].reshape(1, -1)], axis=0))
    w_first = jnp.concatenate([m_intra, w_state[0], w_state[1]], axis=1).astype(BF16)
    w_second = jnp.concatenate([w_out[0], w_out[1]], axis=0).astype(BF16)
    return w_first, w_second, a_chunk[0], a_chunk[1]


SCAN_LANES = 256


def _s5_scan_kernel(lf_ref, lb_ref, af_ref, ab_ref, of_ref, ob_ref, *, nchunk):
    afr, afi = af_ref[0], af_ref[1]
    abr, abi = ab_ref[0], ab_ref[1]
    zero = jnp.zeros(lf_ref.shape[2:], F32)

    def fwd(c, carry):
        sr, si = carry
        of_ref[c, 0] = sr
        of_ref[c, 1] = si
        return (afr * sr - afi * si + lf_ref[c, 0], afr * si + afi * sr + lf_ref[c, 1])

    lax.fori_loop(0, nchunk, fwd, (zero, zero))

    def bwd(i, carry):
        c = nchunk - 1 - i
        sr, si = carry
        ob_ref[c, 0] = sr
        ob_ref[c, 1] = si
        return (abr * sr - abi * si + lb_ref[c, 0], abr * si + abi * sr + lb_ref[c, 1])

    lax.fori_loop(0, nchunk, bwd, (zero, zero))


def _s5_scan(loc_f, loc_b, a_f, a_b):
    nchunk, _, bsz, width = loc_f.shape
    blk = pl.BlockSpec((nchunk, 2, bsz, SCAN_LANES), lambda j: (0, 0, 0, j))
    ablk = pl.BlockSpec((2, 1, SCAN_LANES), lambda j: (0, 0, j))
    return pl.pallas_call(
        functools.partial(_s5_scan_kernel, nchunk=nchunk),
        grid=(width // SCAN_LANES,),
        in_specs=[blk, blk, ablk, ablk],
        out_specs=[blk, blk],
        out_shape=[jax.ShapeDtypeStruct(loc_f.shape, F32), jax.ShapeDtypeStruct(loc_b.shape, F32)],
        compiler_params=_cparams(("parallel",)),
    )(loc_f, loc_b, a_f, a_b)


def _s5_glu_kernel(y_ref, w_ref, b_ref, o_ref):
    y = y_ref[...]
    g = 0.5 * y * (1.0 + jnp.tanh(math.sqrt(2.0 / math.pi) * (y + 0.044715 * (y * y * y))))
    z = jnp.dot(g.astype(BF16), w_ref[...], preferred_element_type=F32) + b_ref[...]
    c = GROUP_W
    o_ref[...] = z[:, :c] * jax.nn.sigmoid(z[:, c:])


def _s5_glu(y, w_glu_bf16, b_glu, tm=512):
    n, c = y.shape
    return pl.pallas_call(
        _s5_glu_kernel,
        grid=(n // tm,),
        in_specs=[pl.BlockSpec((tm, c), lambda i: (i, 0)),
                  pl.BlockSpec((c, 2 * c), lambda i: (0, 0)),
                  pl.BlockSpec((1, 2 * c), lambda i: (0, 0))],
        out_specs=pl.BlockSpec((tm, c), lambda i: (i, 0)),
        out_shape=jax.ShapeDtypeStruct((n, c), F32),
        compiler_params=_cparams(("parallel",)),
    )(y, w_glu_bf16, b_glu.reshape(1, 2 * c))


def _s5(u, tables, w_glu_bf16, b_glu):
    bsz, seq, c = u.shape
    w_first, w_second, a_f, a_b = tables
    nchunk = seq // S5_CHUNK
    row = S5_CHUNK * c
    nstate = 2 * S5_GROUPS * S5_STATE
    u2 = u.reshape(bsz * nchunk, row).astype(BF16)
    first = _matmul(u2, w_first, tm=512, tn=512)
    y_intra = first[:, :row]
    to_scan = lambda a: a.reshape(bsz, nchunk, 2, nstate // 2).transpose(1, 2, 0, 3)
    from_scan = lambda a: a.transpose(2, 0, 1, 3).reshape(bsz * nchunk, nstate)
    s_f, s_b = _s5_scan(to_scan(first[:, row:row + nstate]), to_scan(first[:, row + nstate:]), a_f, a_b)
    carried = jnp.concatenate([from_scan(s_f), from_scan(s_b)], axis=1).astype(BF16)
    y = _matmul(carried, w_second, tm=512, tn=512, residual=y_intra)
    out = _s5_glu(y.reshape(bsz * seq, c), w_glu_bf16, b_glu)
    return out.reshape(bsz, seq, c)


def _rope_tables(seq):
    rows = seq // GRID_W
    row = jnp.repeat(jnp.arange(rows, dtype=F32), GRID_W)
    col = jnp.tile(jnp.arange(GRID_W, dtype=F32), rows)
    inv = ROPE_THETA ** (-jnp.arange(0, ROPE_AXIS_DIM, 2, dtype=F32) / ROPE_AXIS_DIM)
    ang_r = row[:, None] * inv
    ang_c = col[:, None] * inv
    cos_h = jnp.concatenate([jnp.cos(ang_r), jnp.cos(ang_r), jnp.cos(ang_c), jnp.cos(ang_c)], axis=1)
    sin_h = jnp.concatenate([-jnp.sin(ang_r), jnp.sin(ang_r), -jnp.sin(ang_c), jnp.sin(ang_c)], axis=1)
    return jnp.tile(cos_h, (1, N_Q_HEADS)), jnp.tile(sin_h, (1, N_Q_HEADS))


def _segment_mean_sq(x, ones_ref):
    sq = x * x
    hi = sq.astype(BF16)
    lo = (sq - hi.astype(F32)).astype(BF16)
    ones = ones_ref[...]
    total = jnp.dot(hi, ones, preferred_element_type=F32) + jnp.dot(lo, ones, preferred_element_type=F32)
    return total * (1.0 / HEAD_DIM)


def _rope(x, cos, sin):
    width = x.shape[1]
    half = ROPE_AXIS_DIM // 2
    lane = lax.broadcasted_iota(jnp.int32, x.shape, 1)
    up = pltpu.roll(x, width - half, 1)
    down = pltpu.roll(x, half, 1)
    partner = jnp.where((lane & (ROPE_AXIS_DIM - 1)) < half, up, down)
    return x * cos + partner * sin


def _qkprep_kernel(q_ref, k_ref, v_ref, cos_ref, sin_ref, cosk_ref, sink_ref, qg_ref, kg_ref,
                   ones_ref, onesk_ref, qo_ref, ko_ref, vo_ref):
    q = q_ref[0]
    qn = q * lax.rsqrt(_segment_mean_sq(q, ones_ref) + RMS_EPS) * qg_ref[...]
    qo_ref[0] = (_rope(qn, cos_ref[...], sin_ref[...]) * (HEAD_DIM ** -0.5)).astype(BF16)
    k = k_ref[0]
    kn = k * lax.rsqrt(_segment_mean_sq(k, onesk_ref) + RMS_EPS) * kg_ref[...]
    kr = _rope(kn, cosk_ref[...], sink_ref[...])
    v = v_ref[0]
    for h in range(N_KV_HEADS):
        ko_ref[0, h] = kr[:, h * HEAD_DIM:(h + 1) * HEAD_DIM].astype(BF16)
        vo_ref[0, h] = v[:, h * HEAD_DIM:(h + 1) * HEAD_DIM].astype(BF16)


def _qkprep(uq, uk, uv, cos, sin, q_g, k_g, tl=512):
    bsz, seq, wq = uq.shape
    wk = uk.shape[2]
    ones = jnp.kron(jnp.eye(N_Q_HEADS, dtype=F32), jnp.ones((HEAD_DIM, HEAD_DIM), F32)).astype(BF16)
    ones_k = ones[:wk, :wk]
    return pl.pallas_call(
        _qkprep_kernel,
        grid=(bsz, seq // tl),
        in_specs=[pl.BlockSpec((1, tl, wq), lambda b, i: (b, i, 0)),
                  pl.BlockSpec((1, tl, wk), lambda b, i: (b, i, 0)),
                  pl.BlockSpec((1, tl, wk), lambda b, i: (b, i, 0)),
                  pl.BlockSpec((tl, wq), lambda b, i: (i, 0)),
                  pl.BlockSpec((tl, wq), lambda b, i: (i, 0)),
                  pl.BlockSpec((tl, wk), lambda b, i: (i, 0)),
                  pl.BlockSpec((tl, wk), lambda b, i: (i, 0)),
                  pl.BlockSpec((1, wq), lambda b, i: (0, 0)),
                  pl.BlockSpec((1, wk), lambda b, i: (0, 0)),
                  pl.BlockSpec((wq, wq), lambda b, i: (0, 0)),
                  pl.BlockSpec((wk, wk), lambda b, i: (0, 0))],
        out_specs=[pl.BlockSpec((1, tl, wq), lambda b, i: (b, i, 0)),
                   pl.BlockSpec((1, N_KV_HEADS, tl, HEAD_DIM), lambda b, i: (b, 0, i, 0)),
                   pl.BlockSpec((1, N_KV_HEADS, tl, HEAD_DIM), lambda b, i: (b, 0, i, 0))],
        out_shape=[jax.ShapeDtypeStruct((bsz, seq, wq), BF16),
                   jax.ShapeDtypeStruct((bsz, N_KV_HEADS, seq, HEAD_DIM), BF16),
                   jax.ShapeDtypeStruct((bsz, N_KV_HEADS, seq, HEAD_DIM), BF16)],
        compiler_params=_cparams(("parallel", "parallel")),
    )(uq, uk, uv, cos, sin, cos, sin, jnp.tile(q_g, N_Q_HEADS).reshape(1, wq),
      jnp.tile(k_g, N_KV_HEADS).reshape(1, wk), ones, ones_k)


def _attn_kernel(q_ref, k_ref, v_ref, o_ref):
    k = k_ref[0, 0]
    v = v_ref[0, 0]
    q2 = q_ref[0]
    outs = []
    for h in range(N_Q_HEADS // N_KV_HEADS):
        q = q2[:, h * HEAD_DIM:(h + 1) * HEAD_DIM]
        s = lax.dot_general(q, k, (((1,), (1,)), ((), ())), preferred_element_type=F32)
        m = jnp.max(s, axis=-1, keepdims=True)
        p = jnp.exp(s - m)
        l = jnp.sum(p, axis=-1, keepdims=True)
        o = jnp.dot(p.astype(BF16), v, preferred_element_type=F32)
        outs.append(o / l)
    o_ref[0] = jnp.concatenate(outs, axis=1)


def _attention(q, k, v, tq=256):
    bsz, seq, wq = q.shape
    per_kv = wq // N_KV_HEADS
    return pl.pallas_call(
        _attn_kernel,
        grid=(bsz, N_KV_HEADS, seq // tq),
        in_specs=[pl.BlockSpec((1, tq, per_kv), lambda b, h, i: (b, i, h)),
                  pl.BlockSpec((1, 1, seq, HEAD_DIM), lambda b, h, i: (b, h, 0, 0)),
                  pl.BlockSpec((1, 1, seq, HEAD_DIM), lambda b, h, i: (b, h, 0, 0))],
        out_specs=pl.BlockSpec((1, tq, per_kv), lambda b, h, i: (b, i, h)),
        out_shape=jax.ShapeDtypeStruct((bsz, seq, wq), F32),
        compiler_params=_cparams(("parallel", "parallel", "parallel")),
    )(q, k, v)


def _finalize_kernel(a_ref, b_ref, c_ref, d_ref, h_ref, mg_ref, w_ref, g_ref, beta_ref,
                     rwh_ref, rwl_ref, rb_ref, h1_ref, h1b_ref, aff_ref):
    acc = DN_ALPHA * h_ref[0]
    for i, ref in enumerate((a_ref, b_ref, c_ref, d_ref)):
        x = ref[0]
        gain = mg_ref[:, i * GROUP_W:(i + 1) * GROUP_W]
        xn = x * lax.rsqrt(jnp.mean(x * x, axis=-1, keepdims=True) + RMS_EPS) * gain
        acc = acc + jnp.dot(xn.astype(BF16), w_ref[i * GROUP_W:(i + 1) * GROUP_W, :],
                            preferred_element_type=F32)
    h1 = _ln(acc, g_ref[...], beta_ref[...])
    h1_ref[0] = h1
    hi = h1.astype(BF16)
    h1b_ref[0] = hi
    lo = (h1 - hi.astype(F32)).astype(BF16)
    nt = (((1,), (1,)), ((), ()))
    logits = (lax.dot_general(rwh_ref[...], hi, nt, preferred_element_type=F32)
              + lax.dot_general(rwh_ref[...], lo, nt, preferred_element_type=F32)
              + lax.dot_general(rwl_ref[...], hi, nt, preferred_element_type=F32)) + rb_ref[...]
    mx = jnp.max(logits, axis=0, keepdims=True)
    ex = jnp.exp(logits - mx)
    aff_ref[0] = ex / jnp.sum(ex, axis=0, keepdims=True)


def _finalize(outs, h, mix_g, w_out_bf16, ln_g, ln_b, router_w, router_b, tm=512):
    bsz, seq, d = h.shape
    ne = router_w.shape[1]
    rwt = router_w.T
    rwh = rwt.astype(BF16)
    rwl = (rwt - rwh.astype(F32)).astype(BF16)
    grp = pl.BlockSpec((1, tm, GROUP_W), lambda b, i: (b, i, 0))
    full = lambda shape: pl.BlockSpec(shape, lambda b, i: (0,) * len(shape))
    return pl.pallas_call(
        _finalize_kernel,
        grid=(bsz, seq // tm),
        in_specs=[grp, grp, grp, grp,
                  pl.BlockSpec((1, tm, d), lambda b, i: (b, i, 0)),
                  full((1, d)), full((d, d)), full((1, d)), full((1, d)),
                  full((ne, d)), full((ne, d)), full((ne, 1))],
        out_specs=[pl.BlockSpec((1, tm, d), lambda b, i: (b, i, 0)),
                   pl.BlockSpec((1, tm, d), lambda b, i: (b, i, 0)),
                   pl.BlockSpec((1, ne, tm), lambda b, i: (b, 0, i))],
        out_shape=[jax.ShapeDtypeStruct((bsz, seq, d), F32),
                   jax.ShapeDtypeStruct((bsz, seq, d), BF16),
                   jax.ShapeDtypeStruct((bsz, ne, seq), F32)],
        compiler_params=_cparams(("parallel", "parallel")),
    )(*outs, h, mix_g.reshape(1, d), w_out_bf16, ln_g.reshape(1, d), ln_b.reshape(1, d),
      rwh, rwl, router_b.reshape(ne, 1))


ROUTE_LANES = 128


def _route_kernel(aff_ref, tri_ref, pos_ref, gate_ref, *, cap):
    aff = aff_ref[0]
    ne, seq = aff.shape
    bits = pltpu.bitcast(aff, jnp.int32)

    def search(i, lo):
        cand = lo | jnp.left_shift(jnp.int32(1), 30 - i)
        cnt = jnp.sum((bits >= cand).astype(jnp.int32), axis=1, keepdims=True)
        return jnp.where(cnt >= cap, cand, lo)

    thr = lax.fori_loop(0, 31, search, jnp.zeros((ne, 1), jnp.int32))
    gt = bits > thr
    eq = bits == thr
    need = cap - jnp.sum(gt.astype(jnp.int32), axis=1, keepdims=True)
    tri = tri_ref[...]

    def prefix(mask_f):
        out = []
        carry = jnp.zeros((ne, 1), F32)
        for blk in range(seq // ROUTE_LANES):
            mb = mask_f[:, blk * ROUTE_LANES:(blk + 1) * ROUTE_LANES]
            out.append(jnp.dot(mb.astype(BF16), tri, preferred_element_type=F32) + carry)
            carry = carry + jnp.sum(mb, axis=1, keepdims=True)
        return jnp.concatenate(out, axis=1)

    eq_rank = prefix(eq.astype(F32))
    sel = gt | (eq & (eq_rank < need.astype(F32)))
    pos = prefix(sel.astype(F32))
    pos_ref[0] = jnp.where(sel, pos.astype(jnp.int32), -1)
    gate_ref[0] = jnp.where(sel, aff, 0.0)


def _route(aff_t, cap):
    bsz, ne, seq = aff_t.shape
    tri = jnp.triu(jnp.ones((ROUTE_LANES, ROUTE_LANES), F32), k=1).astype(BF16)
    blk = pl.BlockSpec((1, ne, seq), lambda b: (b, 0, 0))
    return pl.pallas_call(
        functools.partial(_route_kernel, cap=cap),
        grid=(bsz,),
        in_specs=[blk, pl.BlockSpec((ROUTE_LANES, ROUTE_LANES), lambda b: (0, 0))],
        out_specs=[blk, blk],
        out_shape=[jax.ShapeDtypeStruct((bsz, ne, seq), jnp.int32),
                   jax.ShapeDtypeStruct((bsz, ne, seq), F32)],
        compiler_params=_cparams(("parallel",)),
    )(aff_t, tri)


def _gather_kernel(pos_ref, h_ref, o_ref, *, cap):
    pos = pos_ref[0]
    slot = lax.broadcasted_iota(jnp.int32, (cap, pos.shape[1]), 0)
    onehot = jnp.where(slot == pos, 1.0, 0.0).astype(BF16)
    o_ref[0, 0] = jnp.dot(onehot, h_ref[0], preferred_element_type=F32).astype(BF16)


def _moe_gather(pos, h_bf16, cap):
    bsz, ne, seq = pos.shape
    d = h_bf16.shape[2]
    return pl.pallas_call(
        functools.partial(_gather_kernel, cap=cap),
        grid=(bsz, ne),
        in_specs=[pl.BlockSpec((1, 1, seq), lambda b, e: (b * ne + e, 0, 0)),
                  pl.BlockSpec((1, seq, d), lambda b, e: (b, 0, 0))],
        out_specs=pl.BlockSpec((1, 1, cap, d), lambda b, e: (b, e, 0, 0)),
        out_shape=jax.ShapeDtypeStruct((bsz, ne, cap, d), BF16),
        compiler_params=_cparams(("parallel", "parallel")),
    )(pos.reshape(bsz * ne, 1, seq), h_bf16)


def _ffn_kernel(x_ref, wg_ref, wu_ref, wd_ref, o_ref):
    x = x_ref[0, 0]
    g = jnp.dot(x, wg_ref[0], preferred_element_type=F32)
    u = jnp.dot(x, wu_ref[0], preferred_element_type=F32)
    hid = (g * jax.nn.sigmoid(g) * u).astype(BF16)
    o_ref[0, 0] = jnp.dot(hid, wd_ref[0], preferred_element_type=F32).astype(BF16)


def _moe_ffn(xs, wg, wu, wd):
    bsz, ne, cap, d = xs.shape
    ff = wg.shape[2]
    return pl.pallas_call(
        _ffn_kernel,
        grid=(ne, bsz),
        in_specs=[pl.BlockSpec((1, 1, cap, d), lambda e, b: (b, e, 0, 0)),
                  pl.BlockSpec((1, d, ff), lambda e, b: (e, 0, 0)),
                  pl.BlockSpec((1, d, ff), lambda e, b: (e, 0, 0)),
                  pl.BlockSpec((1, ff, d), lambda e, b: (e, 0, 0))],
        out_specs=pl.BlockSpec((1, 1, cap, d), lambda e, b: (b, e, 0, 0)),
        out_shape=jax.ShapeDtypeStruct((bsz, ne, cap, d), BF16),
        compiler_params=_cparams(("parallel", "parallel")),
    )(xs, wg, wu, wd)


def _combine_kernel(pos_ref, gate_ref, y_ref, h_ref, g_ref, beta_ref, o_ref, *, cap):
    pos = pos_ref[0]
    gate = gate_ref[0]
    tj, ne = pos.shape
    slot = lax.broadcasted_iota(jnp.int32, (tj, cap), 1)
    acc = DN_ALPHA * h_ref[0]
    for e in range(ne):
        weights = jnp.where(slot == pos[:, e:e + 1], gate[:, e:e + 1], 0.0).astype(BF16)
        acc = acc + jnp.dot(weights, y_ref[0, e], preferred_element_type=F32)
    o_ref[0] = _ln(acc, g_ref[...], beta_ref[...])


def _moe_combine(pos_t, gate_t, y, h, ln_g, ln_b, cap, tj=256):
    bsz, seq, ne = pos_t.shape
    d = h.shape[2]
    return pl.pallas_call(
        functools.partial(_combine_kernel, cap=cap),
        grid=(bsz, seq // tj),
        in_specs=[pl.BlockSpec((1, tj, ne), lambda b, j: (b, j, 0)),
                  pl.BlockSpec((1, tj, ne), lambda b, j: (b, j, 0)),
                  pl.BlockSpec((1, ne, cap, d), lambda b, j: (b, 0, 0, 0)),
                  pl.BlockSpec((1, tj, d), lambda b, j: (b, j, 0)),
                  pl.BlockSpec((1, d), lambda b, j: (0, 0)),
                  pl.BlockSpec((1, d), lambda b, j: (0, 0))],
        out_specs=pl.BlockSpec((1, tj, d), lambda b, j: (b, j, 0)),
        out_shape=jax.ShapeDtypeStruct((bsz, seq, d), F32),
        compiler_params=_cparams(("parallel", "parallel")),
    )(pos_t, gate_t, y, h, ln_g.reshape(1, d), ln_b.reshape(1, d))


def _moe(h1, h1_bf16, aff_t, wg, wu, wd, ln_g, ln_b):
    seq = h1.shape[1]
    cap = EC_CAPACITY_FACTOR * seq // N_EXPERTS
    pos, gate = _route(aff_t, cap)
    xs = _moe_gather(pos, h1_bf16, cap)
    y = _moe_ffn(xs, wg, wu, wd)
    return _moe_combine(pos.transpose(0, 2, 1), gate.transpose(0, 2, 1), y, h1, ln_g, ln_b, cap)


def kernel(x, ln_in_g, ln_in_b, w_in, conv_dw_w, conv_dw_b, conv_ln_g, conv_ln_b, hy_short_w, hy_short_b, hy_f_w1, hy_f_b1, hy_f_freq, hy_f_w2, hy_f_b2, hy_f_w3, hy_bias, s5_lam_re, s5_lam_im, s5_log_step, s5_b_re, s5_b_im, s5_c_re, s5_c_im, s5_d, s5_w_glu, s5_b_glu, q_norm_g, k_norm_g, mix_norm_g, w_out, ln1_g, ln1_b, router_w, router_b, exp_w_gate, exp_w_up, exp_w_down, ln2_g, ln2_b):
    bsz, seq, d = x.shape
    assert 2 * seq == FFT_A * FFT_B, "the Hyena FFT factorisation is fixed for this sequence length"
    n = bsz * seq
    cos, sin = _rope_tables(seq)
    consts = _fft_constants()
    h = _layernorm_rows(x.reshape(n, d), ln_in_g, ln_in_b).reshape(bsz, seq, d)
    for l in range(DEPTH):
        u_conv, u_hy, u_s5, u_q, u_k, u_v = (
            a.reshape(bsz, seq, -1) for a in _inproj(h.reshape(n, d), w_in[l].astype(BF16)))
        out_a = _conformer(u_conv, conv_dw_w[l], conv_dw_b[l], conv_ln_g[l], conv_ln_b[l])
        spectra = _hyena_filter_spectra(seq, hy_f_w1[l], hy_f_b1[l], hy_f_freq[l], hy_f_w2[l], hy_f_b2[l],
                                        hy_f_w3[l])
        out_b = _hyena(u_hy, hy_short_w[l], hy_short_b[l], spectra, hy_bias[l], consts)
        tables = _s5_tables(s5_lam_re[l], s5_lam_im[l], s5_log_step[l], s5_b_re[l], s5_b_im[l],
                            s5_c_re[l], s5_c_im[l], s5_d[l])
        out_c = _s5(u_s5, tables, s5_w_glu[l].astype(BF16), s5_b_glu[l])
        q, k, v = _qkprep(u_q, u_k, u_v, cos, sin, q_norm_g[l], k_norm_g[l])
        out_d = _attention(q, k, v)
        h1, h1_bf16, aff_t = _finalize((out_a, out_b, out_c, out_d), h, mix_norm_g[l], w_out[l].astype(BF16),
                                       ln1_g[l], ln1_b[l], router_w[l], router_b[l])
        h = _moe(h1, h1_bf16, aff_t, exp_w_gate[l].astype(BF16), exp_w_up[l].astype(BF16),
                 exp_w_down[l].astype(BF16), ln2_g[l], ln2_b[l])
    return h
```
